```python
import jax, jax.numpy as jnp
from jax import lax
import numpy as np

D_MODEL = 1024
BATCH = 16
SEQ = 2048
DEPTH = 1
DEC_BATCH = 16
DEC_SEQ = 64
PAST_LEN = 4096

CHUNK = 64
D_MIX = D_MODEL
D_CONV = D_MIX // 2
CONV_WIDTH = 31
HG_HEADS = 4
HG_DK = (D_MIX - D_CONV) // HG_HEADS
HG_DV = HG_DK
D_HG = HG_HEADS * HG_DK
D_IN = 2 * D_CONV + 4 * D_HG
D_FF = 2816
EPS = 1e-6

kernel_name = "conformer_conv_hgrn2_hybrid_step"


def _rmsnorm(x, g):
    xf = x.astype(jnp.float32)
    y = xf * lax.rsqrt(jnp.mean(xf * xf, axis=-1, keepdims=True) + EPS)
    return (y * g.astype(jnp.float32)).astype(x.dtype)


def _layernorm(x, g, b):
    xf = x.astype(jnp.float32)
    mu = jnp.mean(xf, axis=-1, keepdims=True)
    var = jnp.mean(jnp.square(xf - mu), axis=-1, keepdims=True)
    y = (xf - mu) * lax.rsqrt(var + EPS)
    return (y * g.astype(jnp.float32) + b.astype(jnp.float32)).astype(x.dtype)


def _swiglu(h, w1, w3, w2):
    return (jax.nn.silu(h @ w1) * (h @ w3)) @ w2


def _conv_module(u, buf, dw_w, dw_b, ln_g, ln_b):
    xp = jnp.concatenate([buf.astype(u.dtype), u], axis=1)
    y = lax.conv_general_dilated(
        xp, dw_w[:, None, :].astype(u.dtype), window_strides=(1,), padding='VALID',
        dimension_numbers=('NWC', 'WIO', 'NWC'), feature_group_count=D_CONV)
    y = y + dw_b.astype(u.dtype)
    y = jax.nn.silu(_layernorm(y, ln_g, ln_b))
    return y, xp[:, -(CONV_WIDTH - 1):]


def _chunk_scan(q, k, logf, v, s0):
    B, T, H, DK = q.shape
    L = min(CHUNK, T)
    n = T // L

    def split(a):
        return a.reshape(B, n, L, *a.shape[2:]).swapaxes(0, 1)

    causal = jnp.tril(jnp.ones((L, L), dtype=bool))[None, :, :, None, None]

    def step(S, inp):
        qc, kc, fc, vc = inp
        b = jnp.cumsum(fc, axis=1)
        diff = b[:, :, None] - b[:, None, :]
        decay = jnp.exp(jnp.where(causal, diff, -jnp.inf))
        A = jnp.einsum('bthk,bshk,btshk->bhts', qc, kc, decay)
        o = jnp.einsum('bhts,bshv->bthv', A, vc) + jnp.einsum('bthk,bhkv->bthv', qc * jnp.exp(b), S)
        bl = b[:, -1]
        S_new = jnp.exp(bl)[..., None] * S + jnp.einsum('bshk,bshv->bhkv', kc * jnp.exp(bl[:, None] - b), vc)
        return S_new, o

    S, o = lax.scan(step, s0, (split(q), split(k), split(logf), split(v)))
    o = o.swapaxes(0, 1).reshape(B, T, H, v.shape[-1])
    return o, S


def _hgrn2(q_raw, f_raw, i_raw, g_raw, s0, lb, gn):
    B, T, _ = q_raw.shape
    f32 = jnp.float32
    q = jax.nn.silu(q_raw.astype(f32)).reshape(B, T, HG_HEADS, HG_DK)
    fr = f_raw.astype(f32).reshape(B, T, HG_HEADS, HG_DK)
    lbh = lb.reshape(HG_HEADS, HG_DK)
    logf = jnp.log(lbh + (1.0 - lbh) * jax.nn.sigmoid(fr))
    k = (1.0 - lbh) * jax.nn.sigmoid(-fr)
    v = i_raw.astype(f32).reshape(B, T, HG_HEADS, HG_DV)
    o, S = _chunk_scan(q, k, logf, v, s0.astype(f32))
    o = o * lax.rsqrt(jnp.mean(o * o, axis=-1, keepdims=True) + EPS)
    o = o.reshape(B, T, D_HG) * gn.astype(f32) * jax.nn.silu(g_raw.astype(f32))
    return o.astype(q_raw.dtype), S


def _layer(x, conv_buf, s0, lb, n1, f1a, f1b, f1c, nm, w_in, dw_w, dw_b, ln_g, ln_b, gn, w_out, n2, f2a, f2b, f2c):
    x = x + 0.5 * _swiglu(_rmsnorm(x, n1), f1a, f1b, f1c)
    h = _rmsnorm(x, nm)
    p = h @ w_in
    a, gt, qr, fr, ir, gr = jnp.split(
        p, [D_CONV, 2 * D_CONV, 2 * D_CONV + D_HG, 2 * D_CONV + 2 * D_HG, 2 * D_CONV + 3 * D_HG], axis=-1)
    u = a * jax.nn.sigmoid(gt)
    c, new_buf = _conv_module(u, conv_buf, dw_w, dw_b, ln_g, ln_b)
    r, S = _hgrn2(qr, fr, ir, gr, s0, lb, gn)
    x = x + jnp.concatenate([c, r], axis=-1) @ w_out
    x = x + 0.5 * _swiglu(_rmsnorm(x, n2), f2a, f2b, f2c)
    return x, new_buf, S


def setup_inputs(seed: int = 0) -> dict:
    key = jax.random.key(seed)
    ks = jax.random.split(key, 24)
    f32 = jnp.float32

    def nrm(k, shape, scale):
        return jax.random.normal(k, shape, f32) * scale

    def gain(k, shape):
        return 1.0 + 0.02 * jax.random.normal(k, shape, f32)

    return {
        "x_prompt": nrm(ks[0], (BATCH, SEQ, D_MODEL), 1.0),
        "x_sample": nrm(ks[1], (DEC_BATCH, DEC_SEQ, D_MODEL), 1.0),
        "state_conv": nrm(ks[2], (DEPTH, DEC_BATCH, CONV_WIDTH - 1, D_CONV), 0.5),
        "state_hgrn": nrm(ks[3], (DEPTH, DEC_BATCH, HG_HEADS, HG_DK, HG_DV), 0.3),
        "ffn1_norm": gain(ks[4], (DEPTH, D_MODEL)),
        "ffn1_w1": nrm(ks[5], (DEPTH, D_MODEL, D_FF), D_MODEL ** -0.5),
        "ffn1_w3": nrm(ks[6], (DEPTH, D_MODEL, D_FF), D_MODEL ** -0.5),
        "ffn1_w2": nrm(ks[7], (DEPTH, D_FF, D_MODEL), D_FF ** -0.5),
        "mix_norm": gain(ks[8], (DEPTH, D_MODEL)),
        "w_in": nrm(ks[9], (DEPTH, D_MODEL, D_IN), D_MODEL ** -0.5),
        "conv_dw_w": nrm(ks[10], (DEPTH, CONV_WIDTH, D_CONV), CONV_WIDTH ** -0.5),
        "conv_dw_b": nrm(ks[11], (DEPTH, D_CONV), 0.02),
        "conv_ln_g": gain(ks[12], (DEPTH, D_CONV)),
        "conv_ln_b": nrm(ks[13], (DEPTH, D_CONV), 0.02),
        "hg_lb_logits": nrm(ks[14], (DEPTH + 1, D_HG), 0.1),
        "hg_gnorm": gain(ks[15], (DEPTH, D_HG)),
        "w_out": nrm(ks[16], (DEPTH, D_MIX, D_MODEL), D_MIX ** -0.5),
        "ffn2_norm": gain(ks[17], (DEPTH, D_MODEL)),
        "ffn2_w1": nrm(ks[18], (DEPTH, D_MODEL, D_FF), D_MODEL ** -0.5),
        "ffn2_w3": nrm(ks[19], (DEPTH, D_MODEL, D_FF), D_MODEL ** -0.5),
        "ffn2_w2": nrm(ks[20], (DEPTH, D_FF, D_MODEL), D_FF ** -0.5),
        "final_norm": gain(ks[21], (D_MODEL,)),
    }


def reference(x_prompt, x_sample, state_conv, state_hgrn, ffn1_norm, ffn1_w1, ffn1_w3, ffn1_w2,
              mix_norm, w_in, conv_dw_w, conv_dw_b, conv_ln_g, conv_ln_b, hg_lb_logits, hg_gnorm,
              w_out, ffn2_norm, ffn2_w1, ffn2_w3, ffn2_w2, final_norm):
    lb_all = jnp.cumsum(jax.nn.softmax(hg_lb_logits.astype(jnp.float32), axis=0), axis=0)
    yp, ys = x_prompt, x_sample
    conv_p, hgrn_p, conv_s, hgrn_s = [], [], [], []
    for l in range(DEPTH):
        lw = (ffn1_norm[l], ffn1_w1[l], ffn1_w3[l], ffn1_w2[l], mix_norm[l], w_in[l],
              conv_dw_w[l], conv_dw_b[l], conv_ln_g[l], conv_ln_b[l], hg_gnorm[l], w_out[l],
              ffn2_norm[l], ffn2_w1[l], ffn2_w3[l], ffn2_w2[l])
        buf0 = jnp.zeros((x_prompt.shape[0], CONV_WIDTH - 1, D_CONV), x_prompt.dtype)
        s00 = jnp.zeros((x_prompt.shape[0], HG_HEADS, HG_DK, HG_DV), jnp.float32)
        yp, bp, sp = _layer(yp, buf0, s00, lb_all[l], *lw)
        ys, bs, ss = _layer(ys, state_conv[l], state_hgrn[l], lb_all[l], *lw)
        conv_p.append(bp.astype(state_conv.dtype))
        hgrn_p.append(sp.astype(state_hgrn.dtype))
        conv_s.append(bs.astype(state_conv.dtype))
        hgrn_s.append(ss.astype(state_hgrn.dtype))
    y_prompt = _rmsnorm(yp, final_norm)
    y_sample = _rmsnorm(ys, final_norm)
    new_conv_prompt = jnp.stack(conv_p, axis=0)
    new_hgrn_prompt = jnp.stack(hgrn_p, axis=0)
    new_conv_sample = jnp.stack(conv_s, axis=0)
    new_hgrn_sample = jnp.stack(hgrn_s, axis=0)
    return (y_prompt, y_sample, new_conv_prompt, new_hgrn_prompt, new_conv_sample, new_hgrn_sample)
```

```python
import functools

import jax
import jax.numpy as jnp
from jax import lax
from jax.experimental import pallas as pl
from jax.experimental.pallas import tpu as pltpu

F32 = jnp.float32
BF16 = jnp.bfloat16

EPS = 1e-6
D_MODEL = 1024
D_CONV = 512
CONV_WIDTH = 31
CONV_HIST = CONV_WIDTH - 1
HG_HEADS = 4
HG_DK = 128
D_HG = HG_HEADS * HG_DK
D_IN = 2 * D_CONV + 4 * D_HG
D_FF = 2816
CHUNK = 64

V7X_MXU_COLS = 256
V7X_SUBLANES = 8
V7X_VMEM_BYTES = 64 * 1024 * 1024

PA_U, PA_Q, PA_LF, PA_K, PA_V, PA_G = (i * D_HG for i in range(6))
D_PA = 6 * D_HG

FF_SPLITS = ((0, 6 * V7X_MXU_COLS), (6 * V7X_MXU_COLS, D_FF))

ROW_TILE = 256
CONV_ROWS = 32
CONV_PAD = 32
DIAG = 16


def _dot(a, b):
    return jnp.dot(a, b, preferred_element_type=F32)


def _dot_nt(a, b):
    return lax.dot_general(a, b, (((1,), (1,)), ((), ())), preferred_element_type=F32)


def _rms(x, g):
    return x * lax.rsqrt(jnp.mean(x * x, axis=-1, keepdims=True) + EPS) * g


def _swiglu(h, w1_ref, w3_ref, w2_ref):
    acc = None
    for lo, hi in FF_SPLITS:
        a = _dot(h, w1_ref[:, lo:hi])
        b = _dot(h, w3_ref[:, lo:hi])
        g = (jax.nn.silu(a) * b).astype(BF16)
        part = _dot(g, w2_ref[lo:hi, :])
        acc = part if acc is None else acc + part
    return acc


def _front_kernel(x_ref, n1_ref, w1_ref, w3_ref, w2_ref, nm_ref, win_ref, lbl_ref, x1_ref, pa_ref):
    x = x_ref[...]
    h = _rms(x, n1_ref[...]).astype(BF16)
    x1 = x + 0.5 * _swiglu(h, w1_ref, w3_ref, w2_ref)
    x1_ref[...] = x1
    h2 = _rms(x1, nm_ref[...]).astype(BF16)

    lg = lbl_ref[...]
    e = jnp.exp(lg - jnp.max(lg, axis=0, keepdims=True))
    lb = e[0:1] / jnp.sum(e, axis=0, keepdims=True)

    def proj(k):
        return _dot(h2, win_ref[:, k * D_HG:(k + 1) * D_HG])

    pa_ref[:, PA_U:PA_U + D_CONV] = proj(0) * jax.nn.sigmoid(proj(1))
    pa_ref[:, PA_Q:PA_Q + D_HG] = jax.nn.silu(proj(2))
    fr = proj(3)
    pa_ref[:, PA_LF:PA_LF + D_HG] = jnp.log(lb + (1.0 - lb) * jax.nn.sigmoid(fr))
    pa_ref[:, PA_K:PA_K + D_HG] = (1.0 - lb) * jax.nn.sigmoid(-fr)
    pa_ref[:, PA_V:PA_V + D_HG] = proj(4)
    pa_ref[:, PA_G:PA_G + D_HG] = jax.nn.silu(proj(5))


def _back_kernel(x1_ref, mix_ref, wout_ref, n2_ref, w1_ref, w3_ref, w2_ref, fn_ref, y_ref):
    x2 = x1_ref[...] + _dot(mix_ref[...], wout_ref[...])
    h = _rms(x2, n2_ref[...]).astype(BF16)
    x3 = x2 + 0.5 * _swiglu(h, w1_ref, w3_ref, w2_ref)
    y_ref[...] = _rms(x3, fn_ref[...])


def _split3(x):
    hi = x.astype(BF16)
    r = x - hi.astype(F32)
    mid = r.astype(BF16)
    lo = (r - mid.astype(F32)).astype(BF16)
    return hi, mid, lo


def _hgrn_chunk(q, lf, kk, v, gs, gn, st_scr):
    L = CHUNK
    row = lax.broadcasted_iota(jnp.int32, (L, D_HG), 0)
    tri = (lax.broadcasted_iota(jnp.int32, (L, L), 0) >= lax.broadcasted_iota(jnp.int32, (L, L), 1)).astype(BF16)
    hi, mid, lo = _split3(lf)
    b = _dot(tri, hi) + _dot(tri, mid) + _dot(tri, lo)
    bl = b[L - 1:L]
    qb = (q * jnp.exp(b)).astype(BF16)
    kd = (kk * jnp.exp(bl - b)).astype(BF16)
    vb = v.astype(BF16)

    levels = []
    blk = L // 2
    while blk >= DIAG:
        pair = 2 * blk
        ref = jnp.concatenate(
            [jnp.broadcast_to(b[p * pair + blk - 1:p * pair + blk], (pair, D_HG)) for p in range(L // pair)], axis=0)
        in_q = (row & (pair - 1)) >= blk
        qe = jnp.where(in_q, q * jnp.exp(b - ref), 0.0).astype(BF16)
        ke = jnp.where(in_q, 0.0, kk * jnp.exp(ref - b)).astype(BF16)
        levels.append((pair, qe, ke))
        blk //= 2

    rowl = lax.broadcasted_iota(jnp.int32, (DIAG, D_HG), 0)
    lane = lax.broadcasted_iota(jnp.int32, (DIAG, L), 1)
    diag = [[None] * (L // DIAG) for _ in range(HG_HEADS)]
    for kb in range(L // DIAG):
        r0 = kb * DIAG
        bq = b[r0:r0 + DIAG]
        qq = q[r0:r0 + DIAG]
        blocks = [jnp.zeros((DIAG, L), F32) for _ in range(HG_HEADS)]
        for sl in range(DIAG):
            s = r0 + sl
            val = jnp.where(rowl >= sl, qq * jnp.exp(bq - b[s:s + 1]) * kk[s:s + 1], 0.0)
            for h in range(HG_HEADS):
                col = jnp.sum(val[:, h * HG_DK:(h + 1) * HG_DK], axis=-1, keepdims=True)
                blocks[h] = jnp.where(lane == s, col, blocks[h])
        for h in range(HG_HEADS):
            diag[h][kb] = blocks[h]

    ri = lax.broadcasted_iota(jnp.int32, (L, L), 0)
    ci = lax.broadcasted_iota(jnp.int32, (L, L), 1)
    outs = []
    for h in range(HG_HEADS):
        hs = slice(h * HG_DK, (h + 1) * HG_DK)
        a = jnp.concatenate(diag[h], axis=0)
        for pair, qe, ke in levels:
            part = _dot_nt(qe[:, hs], ke[:, hs])
            if pair < L:
                part = jnp.where((ri & -pair) == (ci & -pair), part, 0.0)
            a = a + part
        st = st_scr[h]
        o = _dot(a.astype(BF16), vb[:, hs]) + _dot_nt(qb[:, hs], st.astype(BF16))
        st_scr[h] = st * jnp.exp(bl[:, hs]) + _dot(v[:, hs].T.astype(BF16), kd[:, hs])
        o = o * lax.rsqrt(jnp.mean(o * o, axis=-1, keepdims=True) + EPS)
        outs.append(o)
    o = jnp.concatenate(outs, axis=1)
    return o * gn * gs


def _mixer_kernel(pa_ref, cs_ref, hs_ref, dww_ref, dwb_ref, lng_ref, lnb_ref, gn_ref,
                  mix_ref, nconv_ref, nhg_ref, win_scr, st_scr, *, tt):
    t = pl.program_id(1)
    last = pl.num_programs(1) - 1

    @pl.when(t == 0)
    def _load_state():
        win_scr[0:V7X_SUBLANES, :] = jnp.zeros((V7X_SUBLANES, D_CONV), F32)
        win_scr[CONV_PAD - CONV_HIST:CONV_PAD, :] = cs_ref[0]
        for h in range(HG_HEADS):
            st_scr[h] = hs_ref[0, h].T

    win_scr[CONV_PAD:CONV_PAD + tt, :] = pa_ref[0, :, PA_U:PA_U + D_CONV]
    for rb in range(tt // CONV_ROWS):
        base = rb * CONV_ROWS + CONV_PAD - CONV_HIST
        acc = jnp.broadcast_to(dwb_ref[...], (CONV_ROWS, D_CONV))
        for j in range(CONV_WIDTH):
            acc = acc + win_scr[base + j:base + j + CONV_ROWS, :] * dww_ref[j:j + 1, :]
        mu = jnp.mean(acc, axis=-1, keepdims=True)
        d = acc - mu
        var = jnp.mean(d * d, axis=-1, keepdims=True)
        y = d * lax.rsqrt(var + EPS) * lng_ref[...] + lnb_ref[...]
        mix_ref[0, rb * CONV_ROWS:(rb + 1) * CONV_ROWS, 0:D_CONV] = jax.nn.silu(y).astype(BF16)

    @pl.when(t == last)
    def _store_conv_state():
        nconv_ref[0] = win_scr[tt + CONV_PAD - CONV_HIST:tt + CONV_PAD, :]

    win_scr[0:CONV_PAD, :] = win_scr[tt:tt + CONV_PAD, :]

    gn = gn_ref[...]

    def chunk_body(c, carry):
        r0 = pl.multiple_of(c * CHUNK, CHUNK)
        rows = pl.ds(r0, CHUNK)
        r = _hgrn_chunk(pa_ref[0, rows, PA_Q:PA_Q + D_HG], pa_ref[0, rows, PA_LF:PA_LF + D_HG],
                        pa_ref[0, rows, PA_K:PA_K + D_HG], pa_ref[0, rows, PA_V:PA_V + D_HG],
                        pa_ref[0, rows, PA_G:PA_G + D_HG], gn, st_scr)
        mix_ref[0, rows, D_CONV:D_CONV + D_HG] = r.astype(BF16)
        return carry

    lax.fori_loop(0, tt // CHUNK, chunk_body, 0)

    @pl.when(t == last)
    def _store_hgrn_state():
        for h in range(HG_HEADS):
            nhg_ref[0, h] = st_scr[h].T


def _resident(shape):
    nd = len(shape)
    return pl.BlockSpec(shape, lambda *_: (0,) * nd, pipeline_mode=pl.Buffered(1))


def _front(x, n1, w1, w3, w2, nm, w_in, lbl):
    n = x.shape[0]
    tm = min(ROW_TILE, n)
    row = lambda w: pl.BlockSpec((tm, w), lambda i: (i, 0))
    return pl.pallas_call(
        _front_kernel,
        grid=(n // tm,),
        in_specs=[row(D_MODEL), _resident(n1.shape), _resident(w1.shape), _resident(w3.shape), _resident(w2.shape),
                  _resident(nm.shape), _resident(w_in.shape), _resident(lbl.shape)],
        out_specs=[row(D_MODEL), row(D_PA)],
        out_shape=[jax.ShapeDtypeStruct((n, D_MODEL), F32), jax.ShapeDtypeStruct((n, D_PA), F32)],
        compiler_params=pltpu.CompilerParams(dimension_semantics=("arbitrary",),
                                             vmem_limit_bytes=V7X_VMEM_BYTES * 7 // 8),
        name="front",
    )(x, n1, w1, w3, w2, nm, w_in, lbl)


def _back(x1, mix, w_out, n2, w1, w3, w2, fn):
    n = x1.shape[0]
    tm = min(ROW_TILE, n)
    row = lambda w: pl.BlockSpec((tm, w), lambda i: (i, 0))
    return pl.pallas_call(
        _back_kernel,
        grid=(n // tm,),
        in_specs=[row(D_MODEL), row(D_MODEL), _resident(w_out.shape), _resident(n2.shape), _resident(w1.shape),
                  _resident(w3.shape), _resident(w2.shape), _resident(fn.shape)],
        out_specs=row(D_MODEL),
        out_shape=jax.ShapeDtypeStruct((n, D_MODEL), F32),
        compiler_params=pltpu.CompilerParams(dimension_semantics=("arbitrary",),
                                             vmem_limit_bytes=V7X_VMEM_BYTES * 7 // 8),
        name="back",
    )(x1, mix, w_out, n2, w1, w3, w2, fn)


def _mixer(pa, conv_state, hgrn_state, dw_w, dw_b, ln_g, ln_b, gn):
    bsz, seq, _ = pa.shape
    tt = min(seq, 256)
    seq_blk = lambda w: pl.BlockSpec((1, tt, w), lambda b, t: (b, t, 0))
    per_seq3 = pl.BlockSpec((1, CONV_HIST, D_CONV), lambda b, t: (b, 0, 0))
    per_seq4 = pl.BlockSpec((1, HG_HEADS, HG_DK, HG_DK), lambda b, t: (b, 0, 0, 0))
    const = lambda a: pl.BlockSpec(a.shape, lambda b, t: (0,) * a.ndim)
    return pl.pallas_call(
        functools.partial(_mixer_kernel, tt=tt),
        grid=(bsz, seq // tt),
        in_specs=[seq_blk(D_PA), per_seq3, per_seq4, const(dw_w), const(dw_b), const(ln_g), const(ln_b), const(gn)],
        out_specs=[seq_blk(D_MODEL), per_seq3, per_seq4],
        out_shape=[jax.ShapeDtypeStruct((bsz, seq, D_MODEL), BF16),
                   jax.ShapeDtypeStruct((bsz, CONV_HIST, D_CONV), F32),
                   jax.ShapeDtypeStruct((bsz, HG_HEADS, HG_DK, HG_DK), F32)],
        scratch_shapes=[pltpu.VMEM((tt + CONV_PAD, D_CONV), F32), pltpu.VMEM((HG_HEADS, HG_DK, HG_DK), F32)],
        compiler_params=pltpu.CompilerParams(dimension_semantics=("arbitrary", "arbitrary"),
                                             vmem_limit_bytes=V7X_VMEM_BYTES // 2),
        name="mixer",
    )(pa, conv_state, hgrn_state, dw_w, dw_b, ln_g, ln_b, gn)


def kernel(x_prompt, x_sample, state_conv, state_hgrn, ffn1_norm, ffn1_w1, ffn1_w3, ffn1_w2, mix_norm, w_in,
           conv_dw_w, conv_dw_b, conv_ln_g, conv_ln_b, hg_lb_logits, hg_gnorm, w_out, ffn2_norm, ffn2_w1, ffn2_w3,
           ffn2_w2, final_norm):
    assert ffn1_norm.shape[0] == 1 and hg_lb_logits.shape == (2, D_HG), "single-layer model expected"
    row = lambda v: v.reshape(1, -1).astype(F32)
    mm = lambda w: w[0].astype(BF16)
    n1, nm, n2, fn = row(ffn1_norm), row(mix_norm), row(ffn2_norm), row(final_norm)
    dw_b, ln_g, ln_b, gn = row(conv_dw_b), row(conv_ln_g), row(conv_ln_b), row(hg_gnorm)
    f1 = (mm(ffn1_w1), mm(ffn1_w3), mm(ffn1_w2))
    f2 = (mm(ffn2_w1), mm(ffn2_w3), mm(ffn2_w2))
    w_in_b, w_out_b = mm(w_in), mm(w_out)
    dw_w = conv_dw_w[0].astype(F32)
    lbl = hg_lb_logits.astype(F32)

    def layer(x, conv_state, hgrn_state):
        bsz, seq, d = x.shape
        x1, pa = _front(x.reshape(bsz * seq, d), n1, *f1, nm, w_in_b, lbl)
        mix, new_conv, new_hgrn = _mixer(pa.reshape(bsz, seq, D_PA), conv_state, hgrn_state, dw_w, dw_b, ln_g, ln_b, gn)
        y = _back(x1, mix.reshape(bsz * seq, d), w_out_b, n2, *f2, fn)
        return y.reshape(bsz, seq, d), new_conv[None], new_hgrn[None]

    bp = x_prompt.shape[0]
    zero_conv = jnp.zeros((bp, CONV_HIST, D_CONV), F32)
    zero_hgrn = jnp.zeros((bp, HG_HEADS, HG_DK, HG_DK), F32)
    y_p, conv_p, hgrn_p = layer(x_prompt, zero_conv, zero_hgrn)
    y_s, conv_s, hgrn_s = layer(x_sample, state_conv[0].astype(F32), state_hgrn[0].astype(F32))
    return (y_p, y_s, conv_p.astype(state_conv.dtype), hgrn_p.astype(state_hgrn.dtype),
            conv_s.astype(state_conv.dtype), hgrn_s.astype(state_hgrn.dtype))
```

```python
import functools

import jax
import jax.numpy as jnp
from jax import lax
from jax.experimental import pallas as pl
from jax.experimental.pallas import tpu as pltpu

F32 = jnp.float32
BF16 = jnp.bfloat16

EPS = 1e-6
D_MODEL = 1024
D_CONV = 512
CONV_WIDTH = 31
CONV_HIST = CONV_WIDTH - 1
HG_HEADS = 4
HG_DK = 128
D_HG = HG_HEADS * HG_DK
D_IN = 2 * D_CONV + 4 * D_HG
D_FF = 2816
CHUNK = 64

V7X_MXU_COLS = 256
V7X_SUBLANES = 8
V7X_LANES = 128
V7X_VMEM_BYTES = 64 * 1024 * 1024

PA_U, PA_Q, PA_LF, PA_K, PA_V, PA_G = (i * D_HG for i in range(6))
D_PA = 6 * D_HG

FF_SPLITS = ((0, 6 * V7X_MXU_COLS), (6 * V7X_MXU_COLS, D_FF))

ROW_TILE = 256
CONV_ROWS = 64
CONV_PAD = 32
DIAG = 8


def _dot(a, b):
    return jnp.dot(a, b, preferred_element_type=F32)


def _dot_nt(a, b):
    return lax.dot_general(a, b, (((1,), (1,)), ((), ())), preferred_element_type=F32)


def _rms(x, g):
    return x * lax.rsqrt(jnp.mean(x * x, axis=-1, keepdims=True) + EPS) * g


def _swiglu(h, w1_ref, w3_ref, w2_ref):
    acc = None
    for lo, hi in FF_SPLITS:
        a = _dot(h, w1_ref[:, lo:hi])
        b = _dot(h, w3_ref[:, lo:hi])
        g = (jax.nn.silu(a) * b).astype(BF16)
        part = _dot(g, w2_ref[lo:hi, :])
        acc = part if acc is None else acc + part
    return acc


def _front_kernel(x_ref, n1_ref, w1_ref, w3_ref, w2_ref, nm_ref, win_ref, lbl_ref, x1_ref, pa_ref):
    x = x_ref[...]
    h = _rms(x, n1_ref[...]).astype(BF16)
    x1 = x + 0.5 * _swiglu(h, w1_ref, w3_ref, w2_ref)
    x1_ref[...] = x1
    h2 = _rms(x1, nm_ref[...]).astype(BF16)

    lg = lbl_ref[...]
    e = jnp.exp(lg - jnp.max(lg, axis=0, keepdims=True))
    lb = e[0:1] / jnp.sum(e, axis=0, keepdims=True)

    def proj(k):
        return _dot(h2, win_ref[:, k * D_HG:(k + 1) * D_HG])

    pa_ref[:, PA_U:PA_U + D_CONV] = proj(0) * jax.nn.sigmoid(proj(1))
    pa_ref[:, PA_Q:PA_Q + D_HG] = jax.nn.silu(proj(2))
    fr = proj(3)
    pa_ref[:, PA_LF:PA_LF + D_HG] = jnp.log(lb + (1.0 - lb) * jax.nn.sigmoid(fr))
    pa_ref[:, PA_K:PA_K + D_HG] = (1.0 - lb) * jax.nn.sigmoid(-fr)
    pa_ref[:, PA_V:PA_V + D_HG] = proj(4)
    pa_ref[:, PA_G:PA_G + D_HG] = jax.nn.silu(proj(5))


def _back_kernel(x1_ref, mix_ref, wout_ref, n2_ref, w1_ref, w3_ref, w2_ref, fn_ref, y_ref):
    x2 = x1_ref[...] + _dot(mix_ref[...], wout_ref[...])
    h = _rms(x2, n2_ref[...]).astype(BF16)
    x3 = x2 + 0.5 * _swiglu(h, w1_ref, w3_ref, w2_ref)
    y_ref[...] = _rms(x3, fn_ref[...])


def _split3(x):
    hi = x.astype(BF16)
    r = x - hi.astype(F32)
    mid = r.astype(BF16)
    lo = (r - mid.astype(F32)).astype(BF16)
    return hi, mid, lo


def _hgrn_chunk(q, lf, kk, v, gs, gn, st_scr):
    L = CHUNK
    row = lax.broadcasted_iota(jnp.int32, (L, D_HG), 0)
    tri = (lax.broadcasted_iota(jnp.int32, (L, L), 0) >= lax.broadcasted_iota(jnp.int32, (L, L), 1)).astype(BF16)
    hi, mid, lo = _split3(lf)
    b = _dot(tri, hi) + _dot(tri, mid) + _dot(tri, lo)
    c = b - jnp.log(kk)
    bl = b[L - 1:L]
    qb = (q * jnp.exp(b)).astype(BF16)
    kd = jnp.exp(bl - c).astype(BF16)
    vb = v.astype(BF16)

    levels = []
    blk = L // 2
    while blk >= DIAG:
        pair = 2 * blk
        ref = jnp.concatenate(
            [jnp.broadcast_to(b[p * pair + blk - 1:p * pair + blk], (pair, D_HG)) for p in range(L // pair)], axis=0)
        in_q = (row & (pair - 1)) >= blk
        qe = jnp.where(in_q, q * jnp.exp(b - ref), 0.0).astype(BF16)
        ke = jnp.where(in_q, 0.0, jnp.exp(ref - c)).astype(BF16)
        levels.append((pair, qe, ke))
        blk //= 2

    rowl = lax.broadcasted_iota(jnp.int32, (DIAG, L), 0)
    lane = lax.broadcasted_iota(jnp.int32, (DIAG, L), 1)
    diag = [[None] * (L // DIAG) for _ in range(HG_HEADS)]
    for kb in range(L // DIAG):
        r0 = kb * DIAG
        bq = b[r0:r0 + DIAG]
        qq = q[r0:r0 + DIAG]
        blocks = [jnp.zeros((DIAG, L), F32) for _ in range(HG_HEADS)]
        for sl in range(DIAG):
            s = r0 + sl
            val = qq * jnp.exp(bq - c[s:s + 1])
            keep = (lane == s) & (rowl >= sl)
            for h in range(HG_HEADS):
                col = jnp.sum(val[:, h * HG_DK:(h + 1) * HG_DK], axis=-1, keepdims=True)
                blocks[h] = jnp.where(keep, col, blocks[h])
        for h in range(HG_HEADS):
            diag[h][kb] = blocks[h]

    ri = lax.broadcasted_iota(jnp.int32, (L, L), 0)
    ci = lax.broadcasted_iota(jnp.int32, (L, L), 1)
    outs = []
    for h in range(HG_HEADS):
        hs = slice(h * HG_DK, (h + 1) * HG_DK)
        a = jnp.concatenate(diag[h], axis=0)
        for pair, qe, ke in levels:
            part = _dot_nt(qe[:, hs], ke[:, hs])
            if pair < L:
                part = jnp.where((ri & -pair) == (ci & -pair), part, 0.0)
            a = a + part
        st = st_scr[h]
        o = _dot(a.astype(BF16), vb[:, hs]) + _dot_nt(qb[:, hs], st.astype(BF16))
        st_scr[h] = st * jnp.exp(bl[:, hs]) + _dot(v[:, hs].T.astype(BF16), kd[:, hs])
        o = o * lax.rsqrt(jnp.mean(o * o, axis=-1, keepdims=True) + EPS)
        outs.append(o)
    o = jnp.concatenate(outs, axis=1)
    return o * gn * gs


def _mixer_kernel(pa_ref, cs_ref, hs_ref, dww_ref, dwb_ref, lng_ref, lnb_ref, gn_ref,
                  mix_ref, nconv_ref, nhg_ref, win_scr, st_scr, *, tt):
    t = pl.program_id(1)
    last = pl.num_programs(1) - 1

    @pl.when(t == 0)
    def _load_state():
        win_scr[0:V7X_SUBLANES, :] = jnp.zeros((V7X_SUBLANES, D_CONV), F32)
        win_scr[CONV_PAD - CONV_HIST:CONV_PAD, :] = cs_ref[0]
        for h in range(HG_HEADS):
            st_scr[h] = hs_ref[0, h].T

    win_scr[CONV_PAD:CONV_PAD + tt, :] = pa_ref[0, :, PA_U:PA_U + D_CONV]
    span = CONV_ROWS + CONV_PAD
    first = CONV_PAD - CONV_HIST
    for rb in range(tt // CONV_ROWS):
        cols = []
        for g in range(D_CONV // V7X_LANES):
            lanes = slice(g * V7X_LANES, (g + 1) * V7X_LANES)
            wb = win_scr[rb * CONV_ROWS:rb * CONV_ROWS + span, lanes]
            acc = jnp.broadcast_to(dwb_ref[:, lanes], (CONV_ROWS, V7X_LANES))
            for r in range(V7X_SUBLANES):
                xr = wb if r == 0 else pltpu.roll(wb, span - r, axis=0)
                for a in range(span // V7X_SUBLANES):
                    j = a * V7X_SUBLANES + r - first
                    if 0 <= j < CONV_WIDTH:
                        acc = acc + xr[a * V7X_SUBLANES:a * V7X_SUBLANES + CONV_ROWS] * dww_ref[j:j + 1, lanes]
            cols.append(acc)
        acc = jnp.concatenate(cols, axis=1)
        mu = jnp.mean(acc, axis=-1, keepdims=True)
        d = acc - mu
        var = jnp.mean(d * d, axis=-1, keepdims=True)
        y = d * lax.rsqrt(var + EPS) * lng_ref[...] + lnb_ref[...]
        mix_ref[0, rb * CONV_ROWS:(rb + 1) * CONV_ROWS, 0:D_CONV] = jax.nn.silu(y).astype(BF16)

    win_scr[0:CONV_PAD, :] = win_scr[tt:tt + CONV_PAD, :]

    gn = gn_ref[...]

    for c in range(tt // CHUNK):
        rows = slice(c * CHUNK, (c + 1) * CHUNK)
        r = _hgrn_chunk(pa_ref[0, rows, PA_Q:PA_Q + D_HG], pa_ref[0, rows, PA_LF:PA_LF + D_HG],
                        pa_ref[0, rows, PA_K:PA_K + D_HG], pa_ref[0, rows, PA_V:PA_V + D_HG],
                        pa_ref[0, rows, PA_G:PA_G + D_HG], gn, st_scr)
        mix_ref[0, rows, D_CONV:D_CONV + D_HG] = r.astype(BF16)

    @pl.when(t == last)
    def _store_states():
        nconv_ref[0] = win_scr[tt + CONV_PAD - CONV_HIST:tt + CONV_PAD, :]
        for h in range(HG_HEADS):
            nhg_ref[0, h] = st_scr[h].T


def _resident(shape):
    nd = len(shape)
    return pl.BlockSpec(shape, lambda *_: (0,) * nd, pipeline_mode=pl.Buffered(1))


def _front(x, n1, w1, w3, w2, nm, w_in, lbl):
    n = x.shape[0]
    tm = min(ROW_TILE, n)
    row = lambda w: pl.BlockSpec((tm, w), lambda i: (i, 0))
    return pl.pallas_call(
        _front_kernel,
        grid=(n // tm,),
        in_specs=[row(D_MODEL), _resident(n1.shape), _resident(w1.shape), _resident(w3.shape), _resident(w2.shape),
                  _resident(nm.shape), _resident(w_in.shape), _resident(lbl.shape)],
        out_specs=[row(D_MODEL), row(D_PA)],
        out_shape=[jax.ShapeDtypeStruct((n, D_MODEL), F32), jax.ShapeDtypeStruct((n, D_PA), F32)],
        compiler_params=pltpu.CompilerParams(dimension_semantics=("arbitrary",),
                                             vmem_limit_bytes=V7X_VMEM_BYTES * 7 // 8),
        name="front",
    )(x, n1, w1, w3, w2, nm, w_in, lbl)


def _back(x1, mix, w_out, n2, w1, w3, w2, fn):
    n = x1.shape[0]
    tm = min(ROW_TILE, n)
    row = lambda w: pl.BlockSpec((tm, w), lambda i: (i, 0))
    return pl.pallas_call(
        _back_kernel,
        grid=(n // tm,),
        in_specs=[row(D_MODEL), row(D_MODEL), _resident(w_out.shape), _resident(n2.shape), _resident(w1.shape),
                  _resident(w3.shape), _resident(w2.shape), _resident(fn.shape)],
        out_specs=row(D_MODEL),
        out_shape=jax.ShapeDtypeStruct((n, D_MODEL), F32),
        compiler_params=pltpu.CompilerParams(dimension_semantics=("arbitrary",),
                                             vmem_limit_bytes=V7X_VMEM_BYTES * 7 // 8),
        name="back",
    )(x1, mix, w_out, n2, w1, w3, w2, fn)


def _mixer(pa, conv_state, hgrn_state, dw_w, dw_b, ln_g, ln_b, gn):
    bsz, seq, _ = pa.shape
    tt = min(seq, 256)
    seq_blk = lambda w: pl.BlockSpec((1, tt, w), lambda b, t: (b, t, 0))
    per_seq3 = pl.BlockSpec((1, CONV_HIST, D_CONV), lambda b, t: (b, 0, 0))
    per_seq4 = pl.BlockSpec((1, HG_HEADS, HG_DK, HG_DK), lambda b, t: (b, 0, 0, 0))
    const = lambda a: pl.BlockSpec(a.shape, lambda b, t: (0,) * a.ndim)
    return pl.pallas_call(
        functools.partial(_mixer_kernel, tt=tt),
        grid=(bsz, seq // tt),
        in_specs=[seq_blk(D_PA), per_seq3, per_seq4, const(dw_w), const(dw_b), const(ln_g), const(ln_b), const(gn)],
        out_specs=[seq_blk(D_MODEL), per_seq3, per_seq4],
        out_shape=[jax.ShapeDtypeStruct((bsz, seq, D_MODEL), BF16),
                   jax.ShapeDtypeStruct((bsz, CONV_HIST, D_CONV), F32),
                   jax.ShapeDtypeStruct((bsz, HG_HEADS, HG_DK, HG_DK), F32)],
        scratch_shapes=[pltpu.VMEM((tt + CONV_PAD, D_CONV), F32), pltpu.VMEM((HG_HEADS, HG_DK, HG_DK), F32)],
        compiler_params=pltpu.CompilerParams(dimension_semantics=("arbitrary", "arbitrary"),
                                             vmem_limit_bytes=V7X_VMEM_BYTES // 2),
        name="mixer",
    )(pa, conv_state, hgrn_state, dw_w, dw_b, ln_g, ln_b, gn)


def kernel(x_prompt, x_sample, state_conv, state_hgrn, ffn1_norm, ffn1_w1, ffn1_w3, ffn1_w2, mix_norm, w_in,
           conv_dw_w, conv_dw_b, conv_ln_g, conv_ln_b, hg_lb_logits, hg_gnorm, w_out, ffn2_norm, ffn2_w1, ffn2_w3,
           ffn2_w2, final_norm):
    assert ffn1_norm.shape[0] == 1 and hg_lb_logits.shape == (2, D_HG), "single-layer model expected"
    row = lambda v: v.reshape(1, -1).astype(F32)
    mm = lambda w: w[0].astype(BF16)
    n1, nm, n2, fn = row(ffn1_norm), row(mix_norm), row(ffn2_norm), row(final_norm)
    dw_b, ln_g, ln_b, gn = row(conv_dw_b), row(conv_ln_g), row(conv_ln_b), row(hg_gnorm)
    f1 = (mm(ffn1_w1), mm(ffn1_w3), mm(ffn1_w2))
    f2 = (mm(ffn2_w1), mm(ffn2_w3), mm(ffn2_w2))
    w_in_b, w_out_b = mm(w_in), mm(w_out)
    dw_w = conv_dw_w[0].astype(F32)
    lbl = hg_lb_logits.astype(F32)

    def layer(x, conv_state, hgrn_state):
        bsz, seq, d = x.shape
        x1, pa = _front(x.reshape(bsz * seq, d), n1, *f1, nm, w_in_b, lbl)
        mix, new_conv, new_hgrn = _mixer(pa.reshape(bsz, seq, D_PA), conv_state, hgrn_state, dw_w, dw_b, ln_g, ln_b, gn)
        y = _back(x1, mix.reshape(bsz * seq, d), w_out_b, n2, *f2, fn)
        return y.reshape(bsz, seq, d), new_conv[None], new_hgrn[None]

    bp = x_prompt.shape[0]
    zero_conv = jnp.zeros((bp, CONV_HIST, D_CONV), F32)
    zero_hgrn = jnp.zeros((bp, HG_HEADS, HG_DK, HG_DK), F32)
    y_p, conv_p, hgrn_p = layer(x_prompt, zero_conv, zero_hgrn)
    y_s, conv_s, hgrn_s = layer(x_sample, state_conv[0].astype(F32), state_hgrn[0].astype(F32))
    return (y_p, y_s, conv_p.astype(state_conv.dtype), hgrn_p.astype(state_hgrn.dtype),
            conv_s.astype(state_conv.dtype), hgrn_s.astype(state_hgrn.dtype))
```
